```python
import jax, jax.numpy as jnp
from jax import lax
import numpy as np

D_MODEL = 1024
BATCH = 8
SEQ = 8192
DEPTH = 1
DEC_BATCH = 16
DEC_SEQ = 64
PAST_LEN = 2048

CHUNK = 64
EPS = 1e-6
H_RET = 4
DK_RET = 256
DV_RET = 512
ROPE_BASE = 10000.0
H_ATT = 8
N_KV = 2
HD_ATT = 128
H_IDX = 8
D_IDX = 64
TOPK_MAX = 256
Q_BLOCK = 128
N_EXP = 32
TOP_K = 4
D_FF = 1024
SWIGLU_ALPHA = 1.702
SWIGLU_LIMIT = 7.0
MOE_BLOCK = 128

IN_WIDTHS = (H_RET * DK_RET, H_RET * DK_RET, H_RET * DV_RET, H_RET * DV_RET,
             H_ATT * HD_ATT, N_KV * HD_ATT, N_KV * HD_ATT,
             H_IDX * D_IDX, D_IDX, H_IDX, 2 * D_MODEL)
D_IN = sum(IN_WIDTHS)

kernel_name = 'retention_dsa_moe_stream'


def rms_norm(x, g=None):
    xf = x.astype(jnp.float32)
    y = xf * lax.rsqrt(jnp.mean(xf * xf, axis=-1, keepdims=True) + EPS)
    if g is not None:
        y = y * g.astype(jnp.float32)
    return y.astype(x.dtype)


def rotary(x, pos):
    half = x.shape[-1] // 2
    inv_freq = ROPE_BASE ** (-jnp.arange(half, dtype=jnp.float32) / half)
    ang = pos.astype(jnp.float32)[:, None] * inv_freq[None, :]
    cos = jnp.cos(ang)[None, :, None, :]
    sin = jnp.sin(ang)[None, :, None, :]
    xf = x.astype(jnp.float32)
    x1, x2 = xf[..., :half], xf[..., half:]
    return jnp.concatenate([x1 * cos - x2 * sin, x1 * sin + x2 * cos], axis=-1).astype(x.dtype)


def ret_log_decay():
    return jnp.log(1.0 - 2.0 ** (-5.0 - jnp.arange(H_RET, dtype=jnp.float32)))


def project(x, pos, norm_g, w_in, q_norm_g, k_norm_g, idx_k_norm_g):
    B, L, _ = x.shape
    h = rms_norm(x, norm_g)
    z = h @ w_in
    split_points = [int(v) for v in np.cumsum(IN_WIDTHS)[:-1]]
    rq, rk, rv, rg, aq, ak, av, iq, ik, iw, gates = jnp.split(z, split_points, axis=-1)
    rq = rotary(rq.reshape(B, L, H_RET, DK_RET), pos)
    rk = rotary(rk.reshape(B, L, H_RET, DK_RET), pos) * (DK_RET ** -0.5)
    rv = rv.reshape(B, L, H_RET, DV_RET)
    aq = rms_norm(aq.reshape(B, L, H_ATT, HD_ATT), q_norm_g)
    ak = rms_norm(ak.reshape(B, L, N_KV, HD_ATT), k_norm_g)
    av = av.reshape(B, L, N_KV, HD_ATT)
    iq = iq.reshape(B, L, H_IDX, D_IDX)
    ik = rms_norm(ik, idx_k_norm_g)
    iw = iw * ((H_IDX ** -0.5) * (D_IDX ** -0.5))
    return rq, rk, rv, rg, aq, ak, av, iq, ik, iw, gates


def retention_chunk(s, q, k, v):
    n = q.shape[1]
    lg = ret_log_decay()
    i = jnp.arange(n, dtype=jnp.float32)
    dmat = jnp.exp(lg[:, None, None] * jnp.abs(i[:, None] - i[None, :]))
    scores = jnp.einsum('bchd,bmhd->bhcm', q, k) * dmat
    o_intra = jnp.einsum('bhcm,bmhe->bche', scores, v)
    cross_decay = jnp.exp(lg[None, :] * (i[:, None] + 1.0))
    o_cross = jnp.einsum('bchd,bhde->bche', q, s) * cross_decay[None, :, :, None]
    upd_decay = jnp.exp(lg[None, :] * (n - 1.0 - i)[:, None])
    s_new = (s * jnp.exp(lg * n)[None, :, None, None]
             + jnp.einsum('bchd,bche->bhde', k * upd_decay[None, :, :, None], v))
    return s_new, o_intra + o_cross


def retention_prompt(q, k, v):
    B, L, H, dk = q.shape
    dv = v.shape[-1]
    n_chunks = L // CHUNK
    to_chunks = lambda a: a.reshape(B, n_chunks, CHUNK, *a.shape[2:]).swapaxes(0, 1)
    s0 = jnp.zeros((B, H, dk, dv), jnp.float32)
    s_fin, o = lax.scan(lambda s, xs: retention_chunk(s, *xs), s0,
                        (to_chunks(q), to_chunks(k), to_chunks(v)))
    return s_fin, o.swapaxes(0, 1).reshape(B, L, H, dv)


def dsa_block(q, qi, wi, qpos, k, v, ki, kpos, topk):
    B, n = q.shape[0], q.shape[1]
    s_idx = jnp.einsum('bqhd,bsd->bqsh', qi, ki).astype(jnp.float32)
    score = jnp.einsum('bqsh,bqh->bqs', jax.nn.relu(s_idx), wi.astype(jnp.float32))
    admissible = (kpos[None, :] // CHUNK) <= (qpos[:, None] // CHUNK)
    score = jnp.where(admissible[None], score, -jnp.inf)
    top_val, top_idx = lax.top_k(score, topk)
    valid = jnp.isfinite(top_val)
    k_sel = jax.vmap(lambda kb, ib: kb[ib])(k, top_idx)
    v_sel = jax.vmap(lambda vb, ib: vb[ib])(v, top_idx)
    qg = q.reshape(B, n, N_KV, H_ATT // N_KV, HD_ATT)
    logits = jnp.einsum('bqhgd,bqthd->bqhgt', qg, k_sel).astype(jnp.float32) * (HD_ATT ** -0.5)
    logits = jnp.where(valid[:, :, None, None, :], logits, -jnp.inf)
    p = jax.nn.softmax(logits, axis=-1)
    o = jnp.einsum('bqhgt,bqthd->bqhgd', p.astype(v.dtype), v_sel)
    return o.reshape(B, n, H_ATT * HD_ATT)


def sparse_attention(q, qi, wi, qpos, k, v, ki, kpos, topk):
    B, n = q.shape[0], q.shape[1]
    if n <= Q_BLOCK:
        return dsa_block(q, qi, wi, qpos, k, v, ki, kpos, topk)
    nb = n // Q_BLOCK
    split = lambda a: a.reshape(B, nb, Q_BLOCK, *a.shape[2:]).swapaxes(0, 1)
    o = lax.map(lambda a: dsa_block(a[0], a[1], a[2], a[3], k, v, ki, kpos, topk),
                (split(q), split(qi), split(wi), qpos.reshape(nb, Q_BLOCK)))
    return o.swapaxes(0, 1).reshape(B, n, H_ATT * HD_ATT)


def moe_ffn(h, w_router, b_router, w_gate, b_gate, w_up, b_up, w_down, b_down):
    B, L, D = h.shape
    T = B * L
    xt = h.reshape(T, D)
    logits = (xt @ w_router + b_router).astype(jnp.float32)
    top_logit, top_e = lax.top_k(logits, TOP_K)
    gate = jax.nn.softmax(top_logit, axis=-1)
    A = T * TOP_K
    e_flat = top_e.reshape(A)
    tok_flat = jnp.arange(A, dtype=jnp.int32) // TOP_K
    order = jnp.argsort(e_flat)
    e_sorted = e_flat[order]
    counts = jnp.zeros((N_EXP,), jnp.int32).at[e_flat].add(1)
    padded = (counts + MOE_BLOCK - 1) // MOE_BLOCK * MOE_BLOCK
    start = jnp.cumsum(counts) - counts
    pend = jnp.cumsum(padded)
    pstart = pend - padded
    dest = pstart[e_sorted] + (jnp.arange(A, dtype=jnp.int32) - start[e_sorted])
    n_blocks = -(-(A + N_EXP * (MOE_BLOCK - 1)) // MOE_BLOCK)
    P = n_blocks * MOE_BLOCK
    row_tok = jnp.full((P,), T, jnp.int32).at[dest].set(tok_flat[order])
    block_e = jnp.minimum(jnp.searchsorted(pend, jnp.arange(n_blocks) * MOE_BLOCK, side='right'), N_EXP - 1)
    x_pad = jnp.concatenate([xt, jnp.zeros((1, D), xt.dtype)], axis=0)

    def expert_block(args):
        tok_blk, e = args
        xb = x_pad[tok_blk]
        a = jnp.minimum(xb @ w_gate[e] + b_gate[e], SWIGLU_LIMIT)
        u = jnp.clip(xb @ w_up[e] + b_up[e], -SWIGLU_LIMIT, SWIGLU_LIMIT)
        hid = a * jax.nn.sigmoid(SWIGLU_ALPHA * a) * (u + 1.0)
        return hid @ w_down[e] + b_down[e]

    y_rows = lax.map(expert_block, (row_tok.reshape(n_blocks, MOE_BLOCK), block_e)).reshape(P, D)
    dest_of = jnp.zeros((A,), jnp.int32).at[order].set(dest)
    y = jnp.einsum('tkd,tk->td', y_rows[dest_of].reshape(T, TOP_K, D), gate.astype(y_rows.dtype))
    return y.reshape(B, L, D)


def merge_and_ffn(x, o_ret, rg, o_att, gates, w_ret_out, w_att_out, w_o, norm_ffn_g,
                  w_router, b_router, w_gate, b_gate, w_up, b_up, w_down, b_down):
    B, L, _ = x.shape
    o_ret = rms_norm(o_ret).reshape(B, L, H_RET * DV_RET).astype(x.dtype)
    ret_branch = (jax.nn.silu(rg) * o_ret) @ w_ret_out
    att_branch = o_att @ w_att_out
    g_ret, g_att = jnp.split(gates, 2, axis=-1)
    mixed = jax.nn.sigmoid(g_ret) * ret_branch + jax.nn.sigmoid(g_att) * att_branch
    x = x + mixed @ w_o
    return x + moe_ffn(rms_norm(x, norm_ffn_g), w_router, b_router, w_gate, b_gate, w_up, b_up, w_down, b_down)


def layer_prompt(x, lw):
    L = x.shape[1]
    pos = jnp.arange(L, dtype=jnp.int32)
    rq, rk, rv, rg, aq, ak, av, iq, ik, iw, gates = project(x, pos, *lw[:5])
    s_ret, o_ret = retention_prompt(rq, rk, rv)
    topk = min(TOPK_MAX, L // 4)
    o_att = sparse_attention(aq, iq, iw, pos, ak, av, ik, pos, topk)
    y = merge_and_ffn(x, o_ret, rg, o_att, gates, *lw[5:])
    return y, ak, av, ik, s_ret


def layer_sample(x, ck, cv, cik, s_ret, lw):
    n = x.shape[1]
    past = ck.shape[1]
    pos = past + jnp.arange(n, dtype=jnp.int32)
    rq, rk, rv, rg, aq, ak, av, iq, ik, iw, gates = project(x, pos, *lw[:5])
    s_new, o_ret = retention_chunk(s_ret.astype(jnp.float32), rq, rk, rv)
    k_all = jnp.concatenate([ck.astype(ak.dtype), ak], axis=1)
    v_all = jnp.concatenate([cv.astype(av.dtype), av], axis=1)
    ik_all = jnp.concatenate([cik.astype(ik.dtype), ik], axis=1)
    kpos = jnp.arange(past + n, dtype=jnp.int32)
    topk = min(TOPK_MAX, (past + n) // 4)
    o_att = sparse_attention(aq, iq, iw, pos, k_all, v_all, ik_all, kpos, topk)
    y = merge_and_ffn(x, o_ret, rg, o_att, gates, *lw[5:])
    return y, ak, av, ik, s_new.astype(s_ret.dtype)


def setup_inputs(seed: int = 0) -> dict:
    key = jax.random.key(seed)
    ks = jax.random.split(key, 24)
    f32 = jnp.float32
    nrm = lambda k, shape, scale: jax.random.normal(k, shape, f32) * scale
    gain = lambda k, shape: 1.0 + 0.02 * jax.random.normal(k, shape, f32)
    return {
        'x_prompt': nrm(ks[0], (BATCH, SEQ, D_MODEL), 1.0),
        'x_sample': nrm(ks[1], (DEC_BATCH, DEC_SEQ, D_MODEL), 1.0),
        'cache_att_k': nrm(ks[2], (DEPTH, DEC_BATCH, PAST_LEN, N_KV, HD_ATT), 1.0),
        'cache_att_v': nrm(ks[3], (DEPTH, DEC_BATCH, PAST_LEN, N_KV, HD_ATT), 1.0),
        'cache_idx_k': nrm(ks[4], (DEPTH, DEC_BATCH, PAST_LEN, D_IDX), 1.0),
        'state_ret': nrm(ks[5], (DEPTH, DEC_BATCH, H_RET, DK_RET, DV_RET), 0.5),
        'norm_mix_g': gain(ks[6], (DEPTH, D_MODEL)),
        'w_in': nrm(ks[7], (DEPTH, D_MODEL, D_IN), D_MODEL ** -0.5),
        'q_norm_g': gain(ks[8], (DEPTH, HD_ATT)),
        'k_norm_g': gain(ks[9], (DEPTH, HD_ATT)),
        'idx_k_norm_g': gain(ks[10], (DEPTH, D_IDX)),
        'w_ret_out': nrm(ks[11], (DEPTH, H_RET * DV_RET, D_MODEL), (H_RET * DV_RET) ** -0.5),
        'w_att_out': nrm(ks[12], (DEPTH, H_ATT * HD_ATT, D_MODEL), (H_ATT * HD_ATT) ** -0.5),
        'w_o': nrm(ks[13], (DEPTH, D_MODEL, D_MODEL), D_MODEL ** -0.5),
        'norm_ffn_g': gain(ks[14], (DEPTH, D_MODEL)),
        'w_router': nrm(ks[15], (DEPTH, D_MODEL, N_EXP), D_MODEL ** -0.5),
        'b_router': nrm(ks[16], (DEPTH, N_EXP), 0.01),
        'w_gate': nrm(ks[17], (DEPTH, N_EXP, D_MODEL, D_FF), D_MODEL ** -0.5),
        'b_gate': nrm(ks[18], (DEPTH, N_EXP, D_FF), 0.01),
        'w_up': nrm(ks[19], (DEPTH, N_EXP, D_MODEL, D_FF), D_MODEL ** -0.5),
        'b_up': nrm(ks[20], (DEPTH, N_EXP, D_FF), 0.01),
        'w_down': nrm(ks[21], (DEPTH, N_EXP, D_FF, D_MODEL), D_FF ** -0.5),
        'b_down': nrm(ks[22], (DEPTH, N_EXP, D_MODEL), 0.01),
    }


def reference(x_prompt, x_sample, cache_att_k, cache_att_v, cache_idx_k, state_ret,
              norm_mix_g, w_in, q_norm_g, k_norm_g, idx_k_norm_g, w_ret_out, w_att_out, w_o,
              norm_ffn_g, w_router, b_router, w_gate, b_gate, w_up, b_up, w_down, b_down):
    yp, ys = x_prompt, x_sample
    pk, pv, pik, ps = [], [], [], []
    sk, sv, sik, ss = [], [], [], []
    for l in range(DEPTH):
        lw = (norm_mix_g[l], w_in[l], q_norm_g[l], k_norm_g[l], idx_k_norm_g[l],
              w_ret_out[l], w_att_out[l], w_o[l], norm_ffn_g[l],
              w_router[l], b_router[l], w_gate[l], b_gate[l], w_up[l], b_up[l], w_down[l], b_down[l])
        yp, k_p, v_p, ik_p, s_p = layer_prompt(yp, lw)
        pk.append(k_p); pv.append(v_p); pik.append(ik_p); ps.append(s_p)
        ys, k_s, v_s, ik_s, s_s = layer_sample(ys, cache_att_k[l], cache_att_v[l], cache_idx_k[l], state_ret[l], lw)
        sk.append(k_s); sv.append(v_s); sik.append(ik_s); ss.append(s_s)
    return (yp, ys, jnp.stack(pk), jnp.stack(pv), jnp.stack(pik), jnp.stack(ps),
            jnp.stack(sk), jnp.stack(sv), jnp.stack(sik), jnp.stack(ss))
```

```python
import functools

import numpy as np
import jax
import jax.numpy as jnp
from jax import lax
from jax.experimental import pallas as pl
from jax.experimental.pallas import tpu as pltpu

F32 = jnp.float32
BF16 = jnp.bfloat16
I32 = jnp.int32

D_MODEL = 1024
CHUNK = 64
EPS = 1e-6
H_RET = 4
DK_RET = 256
DV_RET = 512
ROPE_BASE = 10000.0
H_ATT = 8
N_KV = 2
HD_ATT = 128
H_IDX = 8
D_IDX = 64
TOPK_MAX = 256
N_EXP = 32
TOP_K = 4
D_FF = 1024
SWIGLU_ALPHA = 1.702
SWIGLU_LIMIT = 7.0

LANES = 128
VMEM_LIMIT = 56 * 1024 * 1024

INT_MIN = -2 ** 31
NEG_BIG = -1e30

_OFF = np.concatenate([[0], np.cumsum([H_RET * DK_RET, H_RET * DK_RET, H_RET * DV_RET, H_RET * DV_RET,
                                       H_ATT * HD_ATT, N_KV * HD_ATT, N_KV * HD_ATT,
                                       H_IDX * D_IDX, D_IDX, H_IDX, 2 * D_MODEL])]).astype(int)
C_RQ, C_RK, C_RV, C_RG, C_AQ, C_AK, C_AV, C_IQ, C_IK, C_IW, C_GT, C_END = [int(v) for v in _OFF]
P_SM = C_IK
P_GT = P_SM + 2 * LANES
P_END = P_GT + 2 * D_MODEL


def _cparams(sem):
    return pltpu.CompilerParams(dimension_semantics=sem, vmem_limit_bytes=VMEM_LIMIT)


def _const_spec(shape):
    nd = len(shape)
    return pl.BlockSpec(shape, lambda *a: (0,) * nd, pipeline_mode=pl.Buffered(1))


def _proj_kernel(x_ref, g_ref, cos_ref, sin_ref, w_ref, qg_ref, kg_ref, glo_ref, ghi_ref,
                 rq_ref, rk_ref, rv_ref, rg_ref, aq_ref, ak_ref, av_ref, akb_ref, avt_ref,
                 iq_ref, ikw_ref, ik2_ref, gt_ref, *, tk):
    x = x_ref[...]
    ms = jnp.mean(x * x, axis=-1, keepdims=True)
    h = (x * lax.rsqrt(ms + EPS) * g_ref[...]).astype(BF16)
    cos = cos_ref[...]
    sin = sin_ref[...]

    def mm(c0, c1):
        return jnp.dot(h, w_ref[:, c0:c1], preferred_element_type=F32)

    def rot(z):
        x1 = z[:, :LANES]
        x2 = z[:, LANES:]
        return jnp.concatenate([x1 * cos - x2 * sin, x1 * sin + x2 * cos], axis=-1)

    for hd in range(H_RET):
        zq = mm(C_RQ + hd * DK_RET, C_RQ + (hd + 1) * DK_RET)
        rq_ref[:, hd * DK_RET:(hd + 1) * DK_RET] = rot(zq).astype(BF16)
        zk = mm(C_RK + hd * DK_RET, C_RK + (hd + 1) * DK_RET)
        rk_ref[:, hd * DK_RET:(hd + 1) * DK_RET] = (rot(zk) * (DK_RET ** -0.5)).astype(BF16)
    for c in range(0, H_RET * DV_RET, 512):
        rv_ref[:, c:c + 512] = mm(C_RV + c, C_RV + c + 512).astype(BF16)
        rg_ref[:, c:c + 512] = mm(C_RG + c, C_RG + c + 512).astype(BF16)
    for c in range(0, 2 * D_MODEL, 512):
        gt_ref[:, c:c + 512] = mm(P_GT + c, P_GT + c + 512).astype(BF16)

    def headnorm(z, g):
        m = jnp.mean(z * z, axis=-1, keepdims=True)
        return z * lax.rsqrt(m + EPS) * g

    qg = qg_ref[...] * (HD_ATT ** -0.5)
    for c in range(0, H_ATT * HD_ATT, 512):
        z = mm(C_AQ + c, C_AQ + c + 512)
        for j in range(4):
            zh = z[:, j * HD_ATT:(j + 1) * HD_ATT]
            aq_ref[:, c + j * HD_ATT:c + (j + 1) * HD_ATT] = headnorm(zh, qg).astype(BF16)
    zkv = mm(C_AK, C_AV + N_KV * HD_ATT)
    for j in range(N_KV):
        kn = headnorm(zkv[:, j * HD_ATT:(j + 1) * HD_ATT], kg_ref[...])
        ak_ref[:, j * HD_ATT:(j + 1) * HD_ATT] = kn
        akb_ref[:, j * HD_ATT:(j + 1) * HD_ATT] = kn.astype(BF16)
    zv = zkv[:, N_KV * HD_ATT:]
    av_ref[...] = zv
    for t in range(zv.shape[0] // tk):
        avt_ref[t] = zv[t * tk:(t + 1) * tk, :].T.astype(BF16)

    iq_ref[...] = mm(C_IQ, C_IQ + H_IDX * D_IDX).astype(BF16)

    zs = mm(P_SM, P_SM + 2 * LANES)
    za = zs[:, :LANES]
    zb = zs[:, LANES:]
    lane = lax.broadcasted_iota(I32, za.shape, 1)
    zik = jnp.where(lane < D_IDX, za, 0.0)
    r = lax.rsqrt(jnp.sum(zik * zik, axis=-1, keepdims=True) * (1.0 / D_IDX) + EPS)
    ik_lo = zik * r * glo_ref[...]
    ik_hi = jnp.where(lane >= D_IDX, zb, 0.0) * r * ghi_ref[...]
    iw = jnp.where((lane >= D_IDX) & (lane < D_IDX + H_IDX), za, 0.0) * ((H_IDX ** -0.5) * (D_IDX ** -0.5))
    ikw_ref[...] = ik_lo + iw
    ik2_ref[:, :LANES] = ik_lo.astype(BF16)
    ik2_ref[:, LANES:] = ik_hi.astype(BF16)


def _proj_call(x, norm_g, cos_t, sin_t, w_all, qg, kg, glo, ghi, *, tm, tk, npos):
    T = x.shape[0]
    nblk = T // tm
    tok = lambda w: pl.BlockSpec((tm, w), lambda i: (i, 0))
    pos = pl.BlockSpec((tm, LANES), lambda i: (i % npos, 0))
    outs = [
        (jax.ShapeDtypeStruct((T, H_RET * DK_RET), BF16), tok(H_RET * DK_RET)),
        (jax.ShapeDtypeStruct((T, H_RET * DK_RET), BF16), tok(H_RET * DK_RET)),
        (jax.ShapeDtypeStruct((T, H_RET * DV_RET), BF16), tok(H_RET * DV_RET)),
        (jax.ShapeDtypeStruct((T, H_RET * DV_RET), BF16), tok(H_RET * DV_RET)),
        (jax.ShapeDtypeStruct((T, H_ATT * HD_ATT), BF16), tok(H_ATT * HD_ATT)),
        (jax.ShapeDtypeStruct((T, N_KV * HD_ATT), F32), tok(N_KV * HD_ATT)),
        (jax.ShapeDtypeStruct((T, N_KV * HD_ATT), F32), tok(N_KV * HD_ATT)),
        (jax.ShapeDtypeStruct((T, N_KV * HD_ATT), BF16), tok(N_KV * HD_ATT)),
        (jax.ShapeDtypeStruct((T // tk, N_KV * HD_ATT, tk), BF16),
         pl.BlockSpec((tm // tk, N_KV * HD_ATT, tk), lambda i: (i, 0, 0))),
        (jax.ShapeDtypeStruct((T, H_IDX * D_IDX), BF16), tok(H_IDX * D_IDX)),
        (jax.ShapeDtypeStruct((T, LANES), F32), tok(LANES)),
        (jax.ShapeDtypeStruct((T, 2 * LANES), BF16), tok(2 * LANES)),
        (jax.ShapeDtypeStruct((T, 2 * D_MODEL), BF16), tok(2 * D_MODEL)),
    ]
    return pl.pallas_call(
        functools.partial(_proj_kernel, tk=tk),
        grid=(nblk,),
        in_specs=[tok(D_MODEL), _const_spec((1, D_MODEL)), pos, pos, _const_spec((D_MODEL, P_END)),
                  _const_spec((1, HD_ATT)), _const_spec((1, HD_ATT)), _const_spec((1, LANES)), _const_spec((1, LANES))],
        out_specs=[o[1] for o in outs],
        out_shape=[o[0] for o in outs],
        compiler_params=_cparams(("arbitrary",)),
        name="proj",
    )(x, norm_g, cos_t, sin_t, w_all, qg, kg, glo, ghi)


def _ret_kernel(gpow_ref, q_ref, k_ref, v_ref, dm_ref, cd_ref, ud_ref, *rest, has_init):
    if has_init:
        s0_ref, o_ref, sfin_ref, s_ref = rest
    else:
        o_ref, sfin_ref, s_ref = rest
    hd = pl.program_id(1)
    sc = pl.program_id(2)

    @pl.when(sc == 0)
    def _():
        if has_init:
            s_ref[...] = s0_ref[0, 0]
        else:
            s_ref[...] = jnp.zeros_like(s_ref)

    q = q_ref[...]
    k = k_ref[...]
    v = v_ref[...]
    s = s_ref[...]
    scores = lax.dot_general(q, k, (((1,), (1,)), ((), ())), preferred_element_type=F32) * dm_ref[0]
    o = jnp.dot(scores.astype(BF16), v, preferred_element_type=F32)
    cd = cd_ref[0]
    o = o + jnp.dot(q, s.astype(BF16), preferred_element_type=F32) * jnp.concatenate([cd] * (DV_RET // LANES), axis=1)
    ud = ud_ref[0]
    kd = k.astype(F32) * jnp.concatenate([ud] * (DK_RET // LANES), axis=1)
    s_new = s * gpow_ref[hd] + jnp.dot(kd.T.astype(BF16), v, preferred_element_type=F32)
    s_ref[...] = s_new
    o_ref[...] = (o * lax.rsqrt(jnp.mean(o * o, axis=-1, keepdims=True) + EPS)).astype(BF16)

    @pl.when(sc == pl.num_programs(2) - 1)
    def _():
        sfin_ref[0, 0] = s_new


def _ret_tables(C, n_real):
    lg = np.log(1.0 - 2.0 ** (-5.0 - np.arange(H_RET, dtype=np.float64)))
    i = np.arange(C)
    real = i < n_real
    same_or_earlier = (i[None, :] // CHUNK) <= (i[:, None] // CHUNK)
    mask = same_or_earlier & real[None, :] & real[:, None]
    dm = np.exp(lg[:, None, None] * np.abs(i[:, None] - i[None, :])[None]) * mask[None]
    cd = np.exp(lg[:, None] * (i[None, :] + 1.0)) * real[None]
    ud = np.exp(lg[:, None] * (n_real - 1.0 - i)[None, :]) * real[None]
    gp = np.exp(lg * n_real)
    rep = lambda a: np.repeat(a[:, :, None], LANES, axis=2)
    return (jnp.asarray(gp, F32), jnp.asarray(dm, F32), jnp.asarray(rep(cd), F32), jnp.asarray(rep(ud), F32))


def _ret_call(rq, rk, rv, s0, *, B, nsc, C, n_real):
    T = rq.shape[0]
    gp, dm, cd, ud = _ret_tables(C, n_real)
    has_init = s0 is not None
    qk_spec = pl.BlockSpec((C, DK_RET), lambda b, h, s, *_: (b * nsc + s, h))
    v_spec = pl.BlockSpec((C, DV_RET), lambda b, h, s, *_: (b * nsc + s, h))
    tab = lambda w: pl.BlockSpec((1, C, w), lambda b, h, s, *_: (h, 0, 0))
    st_spec = pl.BlockSpec((1, 1, DK_RET, DV_RET), lambda b, h, s, *_: (b, h, 0, 0))
    in_specs = [qk_spec, qk_spec, v_spec, tab(C), tab(LANES), tab(LANES)]
    args = [rq, rk, rv, dm, cd, ud]
    if has_init:
        in_specs.append(st_spec)
        args.append(s0)
    return pl.pallas_call(
        functools.partial(_ret_kernel, has_init=has_init),
        grid_spec=pltpu.PrefetchScalarGridSpec(
            num_scalar_prefetch=1, grid=(B, H_RET, nsc), in_specs=in_specs,
            out_specs=[v_spec, st_spec],
            scratch_shapes=[pltpu.VMEM((DK_RET, DV_RET), F32)]),
        out_shape=[jax.ShapeDtypeStruct((T, H_RET * DV_RET), BF16),
                   jax.ShapeDtypeStruct((B, H_RET, DK_RET, DV_RET), F32)],
        compiler_params=_cparams(("arbitrary", "arbitrary", "arbitrary")),
        name="retention",
    )(gp, *args)


def _dsa_kernel(nt_ref, qpos_ref, aq_ref, iq_ref, ikw_ref, ik2_ref, k_ref, vt_ref, o_ref,
                keys_ref, m_ref, l_ref, acc_ref, *, qb, tk, topk):
    nt = nt_ref[pl.program_id(1)]
    qchunk = qpos_ref[0] >> 6
    wt = ikw_ref[...].T
    iq = iq_ref[...]
    iqs = jnp.concatenate([iq[:, p * LANES:(p + 1) * LANES] for p in range(H_IDX // 2)], axis=0)
    nt_dims = (((1,), (1,)), ((), ()))

    def tile_off(j):
        return pl.multiple_of(j * tk, tk)

    def p1(j, c):
        off = tile_off(j)
        ikt = ik2_ref[pl.ds(off, tk), :]
        r = jnp.concatenate([ikt[:, :LANES], ikt[:, LANES:]], axis=0)
        s = lax.dot_general(r, iqs, nt_dims, preferred_element_type=F32)
        score = jnp.zeros((tk, qb), F32)
        for h in range(H_IDX):
            sh = s[(h % 2) * tk:(h % 2 + 1) * tk, (h // 2) * qb:(h // 2 + 1) * qb]
            score = score + jnp.maximum(sh, 0.0) * wt[D_IDX + h:D_IDX + h + 1, :]
        bits = lax.bitcast_convert_type(score, I32)
        key = bits ^ ((bits >> 31) & 0x7FFFFFFF)
        kpos = off + lax.broadcasted_iota(I32, (tk, qb), 0)
        keys_ref[pl.ds(off, tk), :] = jnp.where((kpos >> 6) <= qchunk, key, INT_MIN)
        return c

    lax.fori_loop(0, nt, p1, 0)

    def count(pred):
        def body(j, acc):
            off = tile_off(j)
            blk = keys_ref[pl.ds(off, tk), :]
            kidx = off + lax.broadcasted_iota(I32, (tk, qb), 0)
            c = jnp.where(pred(blk, kidx), 1, 0)
            return acc + jnp.sum(c.reshape(tk // 8, 8, qb), axis=0)
        acc = lax.fori_loop(0, nt, body, jnp.zeros((8, qb), I32))
        return jnp.sum(acc, axis=0, keepdims=True)

    def bis(i, res):
        cand = res | lax.shift_left(jnp.int32(1), 31 - i)
        t = cand ^ INT_MIN
        c = count(lambda blk, kidx: blk >= t)
        return jnp.where(c >= topk, cand, res)

    res = lax.fori_loop(0, 32, bis, jnp.zeros((1, qb), I32))
    thr = jnp.maximum(res ^ INT_MIN, INT_MIN + 1)
    need = topk - count(lambda blk, kidx: blk > thr)
    n_eq = count(lambda blk, kidx: blk == thr)
    m_ref[0, 0:1, 0:qb] = jnp.full((1, qb), 2.0 ** 30, F32)

    @pl.when(jnp.max(jnp.where(n_eq > need, 1, 0)) > 0)
    def _():
        def jb(i, jl):
            cand = jl | lax.shift_left(jnp.int32(1), 14 - i)
            c = count(lambda blk, kidx: (blk == thr) & (kidx < cand))
            return jnp.where(c < need, cand, jl)
        jl = lax.fori_loop(0, 15, jb, jnp.zeros((1, qb), I32))
        m_ref[0, 0:1, 0:qb] = jl.astype(F32)

    jlast = m_ref[0, 0:1, 0:qb].astype(I32)

    m_ref[...] = jnp.full(m_ref.shape, NEG_BIG, F32)
    l_ref[...] = jnp.zeros(l_ref.shape, F32)
    acc_ref[...] = jnp.zeros(acc_ref.shape, F32)
    aq = aq_ref[...]
    gsz = H_ATT // N_KV
    qs = [jnp.concatenate([aq[:, (g * gsz + hh) * HD_ATT:(g * gsz + hh + 1) * HD_ATT] for hh in range(gsz)], axis=0)
          for g in range(N_KV)]

    def p3(j, c):
        off = tile_off(j)
        key = keys_ref[pl.ds(off, tk), :]
        kidx = off + lax.broadcasted_iota(I32, (tk, qb), 0)
        sel = (key > thr) | ((key == thr) & (kidx <= jlast))
        bias = jnp.where(sel, 0.0, -jnp.inf)
        bias = jnp.concatenate([bias] * gsz, axis=1)
        for g in range(N_KV):
            kt = k_ref[pl.ds(off, tk), g * HD_ATT:(g + 1) * HD_ATT]
            lt = lax.dot_general(kt, qs[g], nt_dims, preferred_element_type=F32) + bias
            m_old = m_ref[g]
            m_new = jnp.maximum(m_old, jnp.max(lt, axis=0, keepdims=True))
            alpha = jnp.exp(m_old - m_new)
            p = jnp.exp(lt - m_new)
            l_ref[g] = alpha * l_ref[g] + jnp.sum(p, axis=0, keepdims=True)
            vt = vt_ref[j, g * HD_ATT:(g + 1) * HD_ATT, :]
            acc_ref[g] = alpha * acc_ref[g] + jnp.dot(vt, p.astype(BF16), preferred_element_type=F32)
            m_ref[g] = m_new
        return c

    lax.fori_loop(0, nt, p3, 0)
    for g in range(N_KV):
        o = acc_ref[g] / jnp.maximum(l_ref[g], 1e-30)
        for hh in range(gsz):
            o_ref[:, (g * gsz + hh) * HD_ATT:(g * gsz + hh + 1) * HD_ATT] = o[:, hh * qb:(hh + 1) * qb].T.astype(BF16)


def _dsa_call(ntiles, qpos, aq, iq, ikw, ik2, akb, avt, *, B, nqb, qb, lk, tk, topk):
    gsz = H_ATT // N_KV
    qrow = lambda w: pl.BlockSpec((qb, w), lambda b, i, *_: (b * nqb + i, 0))
    krow = lambda w: pl.BlockSpec((lk, w), lambda b, i, *_: (b, 0), pipeline_mode=pl.Buffered(1))
    return pl.pallas_call(
        functools.partial(_dsa_kernel, qb=qb, tk=tk, topk=topk),
        grid_spec=pltpu.PrefetchScalarGridSpec(
            num_scalar_prefetch=1, grid=(B, nqb),
            in_specs=[pl.BlockSpec((1, 1, qb), lambda b, i, *_: (i, 0, 0)),
                      qrow(H_ATT * HD_ATT), qrow(H_IDX * D_IDX), qrow(LANES),
                      krow(2 * LANES), krow(N_KV * HD_ATT),
                      pl.BlockSpec((lk // tk, N_KV * HD_ATT, tk), lambda b, i, *_: (b, 0, 0),
                                   pipeline_mode=pl.Buffered(1))],
            out_specs=qrow(H_ATT * HD_ATT),
            scratch_shapes=[pltpu.VMEM((lk, qb), I32),
                            pltpu.VMEM((N_KV, 1, gsz * qb), F32),
                            pltpu.VMEM((N_KV, 1, gsz * qb), F32),
                            pltpu.VMEM((N_KV, HD_ATT, gsz * qb), F32)]),
        out_shape=jax.ShapeDtypeStruct((B * nqb * qb, H_ATT * HD_ATT), BF16),
        compiler_params=_cparams(("arbitrary", "arbitrary")),
        name="dsa",
    )(ntiles, qpos, aq, iq, ikw, ik2, akb, avt)


def _merge_kernel(x_ref, oret_ref, rg_ref, oatt_ref, gt_ref, wro_ref, wao_ref, wo_ref, gf_ref, wr_ref, br_ref,
                  x1_ref, hn_ref, te_ref, tg_ref):
    rg = rg_ref[...].astype(F32)
    a = (rg * jax.nn.sigmoid(rg) * oret_ref[...].astype(F32)).astype(BF16)
    ret_b = jnp.dot(a, wro_ref[...], preferred_element_type=F32)
    att_b = jnp.dot(oatt_ref[...], wao_ref[...], preferred_element_type=F32)
    gt = gt_ref[...].astype(F32)
    mixed = jax.nn.sigmoid(gt[:, :D_MODEL]) * ret_b + jax.nn.sigmoid(gt[:, D_MODEL:]) * att_b
    x1 = x_ref[...] + jnp.dot(mixed.astype(BF16), wo_ref[...], preferred_element_type=F32)
    x1_ref[...] = x1
    hn = (x1 * lax.rsqrt(jnp.mean(x1 * x1, axis=-1, keepdims=True) + EPS) * gf_ref[...]).astype(BF16)
    hn_ref[...] = hn
    logits = jnp.dot(hn, wr_ref[...], preferred_element_type=F32) + br_ref[...]
    lane = lax.broadcasted_iota(I32, logits.shape, 1)
    cur = jnp.where(lane < N_EXP, logits, -jnp.inf)
    te = jnp.zeros(logits.shape, I32)
    tv = jnp.zeros(logits.shape, F32)
    for kk in range(TOP_K):
        mx = jnp.max(cur, axis=-1, keepdims=True)
        idx = jnp.min(jnp.where(cur == mx, lane, LANES), axis=-1, keepdims=True)
        te = jnp.where(lane == kk, idx, te)
        tv = jnp.where(lane == kk, mx, tv)
        cur = jnp.where(lane == idx, -jnp.inf, cur)
    tvm = jnp.where(lane < TOP_K, tv, -jnp.inf)
    e = jnp.exp(tvm - jnp.max(tvm, axis=-1, keepdims=True))
    te_ref[...] = te
    tg_ref[...] = e / jnp.sum(e, axis=-1, keepdims=True)


def _merge_call(x, oret, rg, oatt, gates, wro, wao, wo, gf, wr, br, *, tm):
    T = x.shape[0]
    tok = lambda w: pl.BlockSpec((tm, w), lambda i: (i, 0))
    return pl.pallas_call(
        _merge_kernel,
        grid=(T // tm,),
        in_specs=[tok(D_MODEL), tok(H_RET * DV_RET), tok(H_RET * DV_RET), tok(H_ATT * HD_ATT), tok(2 * D_MODEL),
                  _const_spec((H_RET * DV_RET, D_MODEL)), _const_spec((H_ATT * HD_ATT, D_MODEL)),
                  _const_spec((D_MODEL, D_MODEL)), _const_spec((1, D_MODEL)),
                  _const_spec((D_MODEL, LANES)), _const_spec((1, LANES))],
        out_specs=[tok(D_MODEL), tok(D_MODEL), tok(LANES), tok(LANES)],
        out_shape=[jax.ShapeDtypeStruct((T, D_MODEL), F32), jax.ShapeDtypeStruct((T, D_MODEL), BF16),
                   jax.ShapeDtypeStruct((T, LANES), I32), jax.ShapeDtypeStruct((T, LANES), F32)],
        compiler_params=_cparams(("arbitrary",)),
        name="merge",
    )(x, oret, rg, oatt, gates, wro, wao, wo, gf, wr, br)


def _experts_kernel(be_ref, nact_ref, x_ref, wg_ref, bg_ref, wu_ref, bu_ref, wd_ref, bd_ref, y_ref):
    i = pl.program_id(0)

    @pl.when(i < nact_ref[0])
    def _():
        x = x_ref[...]
        a = jnp.minimum(jnp.dot(x, wg_ref[0], preferred_element_type=F32) + bg_ref[0], SWIGLU_LIMIT)
        u = jnp.clip(jnp.dot(x, wu_ref[0], preferred_element_type=F32) + bu_ref[0], -SWIGLU_LIMIT, SWIGLU_LIMIT)
        hid = (a * jax.nn.sigmoid(SWIGLU_ALPHA * a) * (u + 1.0)).astype(BF16)
        y_ref[...] = jnp.dot(hid, wd_ref[0], preferred_element_type=F32) + bd_ref[0]

    @pl.when(i >= nact_ref[0])
    def _():
        y_ref[...] = jnp.zeros_like(y_ref)


def _experts_call(block_e, nact, xs, wg, bg, wu, bu, wd, bd, *, tm):
    P = xs.shape[0]
    wspec = lambda k, n: pl.BlockSpec((1, k, n), lambda i, be, na: (be[i], 0, 0))
    return pl.pallas_call(
        _experts_kernel,
        grid_spec=pltpu.PrefetchScalarGridSpec(
            num_scalar_prefetch=2, grid=(P // tm,),
            in_specs=[pl.BlockSpec((tm, D_MODEL), lambda i, be, na: (i, 0)),
                      wspec(D_MODEL, D_FF), wspec(1, D_FF), wspec(D_MODEL, D_FF), wspec(1, D_FF),
                      wspec(D_FF, D_MODEL), wspec(1, D_MODEL)],
            out_specs=pl.BlockSpec((tm, D_MODEL), lambda i, be, na: (i, 0))),
        out_shape=jax.ShapeDtypeStruct((P, D_MODEL), F32),
        compiler_params=_cparams(("arbitrary",)),
        name="experts",
    )(block_e, nact, xs, wg, bg, wu, bu, wd, bd)


def _combine_kernel(x1_ref, ys_ref, tg_ref, y_ref):
    tg = tg_ref[...]
    y = x1_ref[...]
    for kk in range(TOP_K):
        y = y + ys_ref[:, kk * D_MODEL:(kk + 1) * D_MODEL] * tg[:, kk:kk + 1]
    y_ref[...] = y


def _combine_call(x1, ysel, tg, *, tm):
    T = x1.shape[0]
    return pl.pallas_call(
        _combine_kernel,
        grid=(T // tm,),
        in_specs=[pl.BlockSpec((tm, D_MODEL), lambda i: (i, 0)),
                  pl.BlockSpec((tm, TOP_K * D_MODEL), lambda i: (i, 0)),
                  pl.BlockSpec((tm, LANES), lambda i: (i, 0))],
        out_specs=pl.BlockSpec((tm, D_MODEL), lambda i: (i, 0)),
        out_shape=jax.ShapeDtypeStruct((T, D_MODEL), F32),
        compiler_params=_cparams(("arbitrary",)),
        name="combine",
    )(x1, ysel, tg)


def _pack_w_in(w_in):
    z = lambda n: jnp.zeros((D_MODEL, n), w_in.dtype)
    w_ik = w_in[:, C_IK:C_IW]
    w_iw = w_in[:, C_IW:C_GT]
    return jnp.concatenate([w_in[:, :C_IK], w_ik, w_iw, z(LANES - D_IDX - H_IDX), z(D_IDX), w_ik, w_in[:, C_GT:]],
                           axis=1).astype(BF16)


def _rope_tables(pos):
    half = DK_RET // 2
    inv_freq = ROPE_BASE ** (-jnp.arange(half, dtype=F32) / half)
    ang = pos.astype(F32)[:, None] * inv_freq[None, :]
    return jnp.cos(ang), jnp.sin(ang)


def _moe(hn, te, tg, wg, bg, wu, bu, wd, bd, *, tm):
    T = hn.shape[0]
    A = T * TOP_K
    e_flat = te[:, :TOP_K].reshape(A)
    onehot = (e_flat[:, None] == jnp.arange(N_EXP, dtype=I32)[None, :]).astype(I32)
    csum = jnp.cumsum(onehot, axis=0)
    counts = csum[-1]
    rank = jnp.sum((csum - onehot) * onehot, axis=1)
    padded = (counts + tm - 1) // tm * tm
    pend = jnp.cumsum(padded)
    pstart = pend - padded
    dest = pstart[e_flat] + rank
    nb = -(-(A + N_EXP * (tm - 1)) // tm)
    P = nb * tm
    row_tok = jnp.zeros((P,), I32).at[dest].set(jnp.arange(A, dtype=I32) // TOP_K)
    block_e = jnp.minimum(jnp.searchsorted(pend, jnp.arange(nb, dtype=I32) * tm, side='right'), N_EXP - 1).astype(I32)
    nact = (pend[-1:] // tm).astype(I32)
    xs = jnp.take(hn, row_tok, axis=0)
    y_rows = _experts_call(block_e, nact, xs, wg, bg, wu, bu, wd, bd, tm=tm)
    return jnp.take(y_rows, dest, axis=0).reshape(T, TOP_K * D_MODEL)


def kernel(x_prompt, x_sample, cache_att_k, cache_att_v, cache_idx_k, state_ret, norm_mix_g, w_in, q_norm_g, k_norm_g,
           idx_k_norm_g, w_ret_out, w_att_out, w_o, norm_ffn_g, w_router, b_router, w_gate, b_gate, w_up, b_up,
           w_down, b_down):
    return _forward(x_prompt, x_sample, cache_att_k, cache_att_v, cache_idx_k, state_ret, norm_mix_g, w_in, q_norm_g,
                    k_norm_g, idx_k_norm_g, w_ret_out, w_att_out, w_o, norm_ffn_g, w_router, b_router, w_gate,
                    b_gate, w_up, b_up, w_down, b_down)


def _forward(x_prompt, x_sample, cache_att_k, cache_att_v, cache_idx_k, state_ret, norm_mix_g, w_in, q_norm_g,
             k_norm_g, idx_k_norm_g, w_ret_out, w_att_out, w_o, norm_ffn_g, w_router, b_router, w_gate, b_gate,
             w_up, b_up, w_down, b_down, *, TM=512, TK=512, C=256, MT=512):
    assert norm_mix_g.shape[0] == 1, "single layer"
    B, L, _ = x_prompt.shape
    Bs, n_s, _ = x_sample.shape
    past = cache_att_k.shape[2]
    QB = LANES

    w_all = _pack_w_in(w_in[0])
    g_mix = norm_mix_g[0][None, :]
    qg = q_norm_g[0][None, :]
    kg = k_norm_g[0][None, :]
    zpad = jnp.zeros((D_IDX,), F32)
    glo = jnp.concatenate([idx_k_norm_g[0], zpad])[None, :]
    ghi = jnp.concatenate([zpad, idx_k_norm_g[0]])[None, :]
    wro = w_ret_out[0].astype(BF16)
    wao = w_att_out[0].astype(BF16)
    wo = w_o[0].astype(BF16)
    gf = norm_ffn_g[0][None, :]
    wr = jnp.pad(w_router[0], ((0, 0), (0, LANES - N_EXP))).astype(BF16)
    br = jnp.pad(b_router[0], (0, LANES - N_EXP))[None, :]
    wg, wu, wd = w_gate[0].astype(BF16), w_up[0].astype(BF16), w_down[0].astype(BF16)
    bg, bu, bd = b_gate[0][:, None, :], b_up[0][:, None, :], b_down[0][:, None, :]

    Tp = B * L
    xp = x_prompt.reshape(Tp, D_MODEL)
    cos_p, sin_p = _rope_tables(jnp.arange(L, dtype=I32))
    (rq, rk, rv, rg, aq, ak, av, akb, avt, iq, ikw, ik2, gates) = _proj_call(
        xp, g_mix, cos_p, sin_p, w_all, qg, kg, glo, ghi, tm=TM, tk=TK, npos=L // TM)
    oret, s_p = _ret_call(rq, rk, rv, None, B=B, nsc=L // C, C=C, n_real=C)
    nqb = L // QB
    qpos = jnp.arange(L, dtype=I32).reshape(nqb, 1, QB)
    lim = (qpos[:, 0, -1] // CHUNK + 1) * CHUNK
    ntiles = ((lim + TK - 1) // TK).astype(I32)
    topk = min(TOPK_MAX, L // 4)
    oatt = _dsa_call(ntiles, qpos, aq, iq, ikw, ik2, akb, avt, B=B, nqb=nqb, qb=QB, lk=L, tk=TK, topk=topk)
    x1p, hnp, tep, tgp = _merge_call(xp, oret, rg, oatt, gates, wro, wao, wo, gf, wr, br, tm=TM)

    Ts = Bs * n_s
    xs_ = x_sample.reshape(Ts, D_MODEL)
    TMs = min(TM, Ts)
    cos_s, sin_s = _rope_tables(past + jnp.arange(n_s, dtype=I32))
    rep = TMs // n_s
    TKs = 128
    (rq_s, rk_s, rv_s, rg_s, aq_s, ak_s, av_s, akb_s, _, iq_s, ikw_s, ik2_s, gates_s) = _proj_call(
        xs_, g_mix, jnp.tile(cos_s, (rep, 1)), jnp.tile(sin_s, (rep, 1)), w_all, qg, kg, glo, ghi,
        tm=TMs, tk=TKs, npos=1)
    padq = lambda a: jnp.pad(a.reshape(Bs, n_s, -1), ((0, 0), (0, QB - n_s), (0, 0))).reshape(Bs * QB, -1)
    oret_s, s_s = _ret_call(padq(rq_s), padq(rk_s), padq(rv_s), state_ret[0], B=Bs, nsc=1, C=QB, n_real=n_s)
    oret_s = oret_s.reshape(Bs, QB, -1)[:, :n_s].reshape(Ts, -1)
    lk_true = past + n_s
    lk_s = -(-lk_true // TK) * TK
    kpad = lambda a: jnp.pad(a, ((0, 0), (0, lk_s - lk_true), (0, 0)))
    k_all = kpad(jnp.concatenate([cache_att_k[0].reshape(Bs, past, -1).astype(BF16),
                                  akb_s.reshape(Bs, n_s, -1)], axis=1))
    v_all = kpad(jnp.concatenate([cache_att_v[0].reshape(Bs, past, -1).astype(BF16),
                                  av_s.reshape(Bs, n_s, -1).astype(BF16)], axis=1))
    cik = cache_idx_k[0].astype(BF16)
    zc = jnp.zeros_like(cik)
    ik2_all = kpad(jnp.concatenate([jnp.concatenate([cik, zc, zc, cik], axis=-1),
                                    ik2_s.reshape(Bs, n_s, -1)], axis=1))
    avt_all = v_all.reshape(Bs, lk_s // TK, TK, -1).transpose(0, 1, 3, 2).reshape(Bs * (lk_s // TK), -1, TK)
    qpos_s = jnp.concatenate([past + jnp.arange(n_s, dtype=I32), jnp.full((QB - n_s,), -1, I32)]).reshape(1, 1, QB)
    ntiles_s = jnp.full((1,), lk_s // TK, I32)
    topk_s = min(TOPK_MAX, lk_true // 4)
    oatt_s = _dsa_call(ntiles_s, qpos_s, padq(aq_s), padq(iq_s), padq(ikw_s),
                       ik2_all.reshape(Bs * lk_s, -1), k_all.reshape(Bs * lk_s, -1), avt_all,
                       B=Bs, nqb=1, qb=QB, lk=lk_s, tk=TK, topk=topk_s)
    oatt_s = oatt_s.reshape(Bs, QB, -1)[:, :n_s].reshape(Ts, -1)
    x1s, hns, tes, tgs = _merge_call(xs_, oret_s, rg_s, oatt_s, gates_s, wro, wao, wo, gf, wr, br, tm=TMs)

    ysel = _moe(jnp.concatenate([hnp, hns], axis=0), jnp.concatenate([tep, tes], axis=0),
                jnp.concatenate([tgp, tgs], axis=0), wg, bg, wu, bu, wd, bd, tm=MT)
    yp = _combine_call(x1p, ysel[:Tp], tgp, tm=TM)
    ys = _combine_call(x1s, ysel[Tp:], tgs, tm=TMs)

    return (yp.reshape(B, L, D_MODEL), ys.reshape(Bs, n_s, D_MODEL),
            ak.reshape(1, B, L, N_KV, HD_ATT), av.reshape(1, B, L, N_KV, HD_ATT),
            ikw[:, :D_IDX].reshape(1, B, L, D_IDX), s_p[None],
            ak_s.reshape(1, Bs, n_s, N_KV, HD_ATT), av_s.reshape(1, Bs, n_s, N_KV, HD_ATT),
            ikw_s[:, :D_IDX].reshape(1, Bs, n_s, D_IDX), s_s[None])
```

```python
import functools

import numpy as np
import jax
import jax.numpy as jnp
from jax import lax
from jax.experimental import pallas as pl
from jax.experimental.pallas import tpu as pltpu

F32 = jnp.float32
BF16 = jnp.bfloat16
I32 = jnp.int32

D_MODEL = 1024
CHUNK = 64
EPS = 1e-6
H_RET = 4
DK_RET = 256
DV_RET = 512
ROPE_BASE = 10000.0
H_ATT = 8
N_KV = 2
HD_ATT = 128
H_IDX = 8
D_IDX = 64
TOPK_MAX = 256
N_EXP = 32
TOP_K = 4
D_FF = 1024
SWIGLU_ALPHA = 1.702
SWIGLU_LIMIT = 7.0

LANES = 128
VMEM_LIMIT = 56 * 1024 * 1024

INT_MIN = -2 ** 31
NEG_BIG = -1e30
LOG2E = 1.4426950408889634

_OFF = np.concatenate([[0], np.cumsum([H_RET * DK_RET, H_RET * DK_RET, H_RET * DV_RET, H_RET * DV_RET,
                                       H_ATT * HD_ATT, N_KV * HD_ATT, N_KV * HD_ATT,
                                       H_IDX * D_IDX, D_IDX, H_IDX, 2 * D_MODEL])]).astype(int)
C_RQ, C_RK, C_RV, C_RG, C_AQ, C_AK, C_AV, C_IQ, C_IK, C_IW, C_GT, C_END = [int(v) for v in _OFF]
P_SM = C_IK
P_GT = P_SM + 2 * LANES
P_END = P_GT + 2 * D_MODEL


def _cparams(sem):
    return pltpu.CompilerParams(dimension_semantics=sem, vmem_limit_bytes=VMEM_LIMIT)


def _const_spec(shape):
    nd = len(shape)
    return pl.BlockSpec(shape, lambda *a: (0,) * nd, pipeline_mode=pl.Buffered(1))


def _proj_kernel(x_ref, g_ref, cos_ref, sin_ref, w_ref, qg_ref, kg_ref, glo_ref, ghi_ref,
                 rq_ref, rk_ref, rv_ref, rg_ref, aq_ref, ak_ref, av_ref, akb_ref, avt_ref,
                 iq_ref, ikw_ref, ik2_ref, gt_ref, *, tk):
    x = x_ref[...]
    ms = jnp.mean(x * x, axis=-1, keepdims=True)
    h = (x * lax.rsqrt(ms + EPS) * g_ref[...]).astype(BF16)
    cos = cos_ref[...]
    sin = sin_ref[...]

    def mm(c0, c1):
        return jnp.dot(h, w_ref[:, c0:c1], preferred_element_type=F32)

    def rot(z):
        x1 = z[:, :LANES]
        x2 = z[:, LANES:]
        return jnp.concatenate([x1 * cos - x2 * sin, x1 * sin + x2 * cos], axis=-1)

    for hd in range(H_RET):
        zq = mm(C_RQ + hd * DK_RET, C_RQ + (hd + 1) * DK_RET)
        rq_ref[:, hd * DK_RET:(hd + 1) * DK_RET] = rot(zq).astype(BF16)
        zk = mm(C_RK + hd * DK_RET, C_RK + (hd + 1) * DK_RET)
        rk_ref[:, hd * DK_RET:(hd + 1) * DK_RET] = (rot(zk) * (DK_RET ** -0.5)).astype(BF16)
    for c in range(0, H_RET * DV_RET, 512):
        rv_ref[:, c:c + 512] = mm(C_RV + c, C_RV + c + 512).astype(BF16)
        rg_ref[:, c:c + 512] = mm(C_RG + c, C_RG + c + 512).astype(BF16)
    for c in range(0, 2 * D_MODEL, 512):
        gt_ref[:, c:c + 512] = mm(P_GT + c, P_GT + c + 512).astype(BF16)

    def headnorm(z, g):
        m = jnp.mean(z * z, axis=-1, keepdims=True)
        return z * lax.rsqrt(m + EPS) * g

    qg = qg_ref[...] * (HD_ATT ** -0.5 * LOG2E)
    for c in range(0, H_ATT * HD_ATT, 512):
        z = mm(C_AQ + c, C_AQ + c + 512)
        for j in range(4):
            zh = z[:, j * HD_ATT:(j + 1) * HD_ATT]
            aq_ref[:, c + j * HD_ATT:c + (j + 1) * HD_ATT] = headnorm(zh, qg).astype(BF16)
    zkv = mm(C_AK, C_AV + N_KV * HD_ATT)
    for j in range(N_KV):
        kn = headnorm(zkv[:, j * HD_ATT:(j + 1) * HD_ATT], kg_ref[...])
        ak_ref[:, j * HD_ATT:(j + 1) * HD_ATT] = kn
        akb_ref[:, j * HD_ATT:(j + 1) * HD_ATT] = kn.astype(BF16)
    zv = zkv[:, N_KV * HD_ATT:]
    av_ref[...] = zv
    for t in range(zv.shape[0] // tk):
        avt_ref[t] = zv[t * tk:(t + 1) * tk, :].T.astype(BF16)

    iq_ref[...] = mm(C_IQ, C_IQ + H_IDX * D_IDX).astype(BF16)

    zs = mm(P_SM, P_SM + 2 * LANES)
    za = zs[:, :LANES]
    zb = zs[:, LANES:]
    lane = lax.broadcasted_iota(I32, za.shape, 1)
    zik = jnp.where(lane < D_IDX, za, 0.0)
    r = lax.rsqrt(jnp.sum(zik * zik, axis=-1, keepdims=True) * (1.0 / D_IDX) + EPS)
    ik_lo = zik * r * glo_ref[...]
    ik_hi = jnp.where(lane >= D_IDX, zb, 0.0) * r * ghi_ref[...]
    iw = jnp.where((lane >= D_IDX) & (lane < D_IDX + H_IDX), za, 0.0) * ((H_IDX ** -0.5) * (D_IDX ** -0.5))
    ikw_ref[...] = ik_lo + iw
    ik2_ref[:, :LANES] = ik_lo.astype(BF16)
    ik2_ref[:, LANES:] = ik_hi.astype(BF16)


def _proj_call(x, norm_g, cos_t, sin_t, w_all, qg, kg, glo, ghi, *, tm, tk, npos):
    T = x.shape[0]
    nblk = T // tm
    tok = lambda w: pl.BlockSpec((tm, w), lambda i: (i, 0))
    pos = pl.BlockSpec((tm, LANES), lambda i: (i % npos, 0))
    outs = [
        (jax.ShapeDtypeStruct((T, H_RET * DK_RET), BF16), tok(H_RET * DK_RET)),
        (jax.ShapeDtypeStruct((T, H_RET * DK_RET), BF16), tok(H_RET * DK_RET)),
        (jax.ShapeDtypeStruct((T, H_RET * DV_RET), BF16), tok(H_RET * DV_RET)),
        (jax.ShapeDtypeStruct((T, H_RET * DV_RET), BF16), tok(H_RET * DV_RET)),
        (jax.ShapeDtypeStruct((T, H_ATT * HD_ATT), BF16), tok(H_ATT * HD_ATT)),
        (jax.ShapeDtypeStruct((T, N_KV * HD_ATT), F32), tok(N_KV * HD_ATT)),
        (jax.ShapeDtypeStruct((T, N_KV * HD_ATT), F32), tok(N_KV * HD_ATT)),
        (jax.ShapeDtypeStruct((T, N_KV * HD_ATT), BF16), tok(N_KV * HD_ATT)),
        (jax.ShapeDtypeStruct((T // tk, N_KV * HD_ATT, tk), BF16),
         pl.BlockSpec((tm // tk, N_KV * HD_ATT, tk), lambda i: (i, 0, 0))),
        (jax.ShapeDtypeStruct((T, H_IDX * D_IDX), BF16), tok(H_IDX * D_IDX)),
        (jax.ShapeDtypeStruct((T, LANES), F32), tok(LANES)),
        (jax.ShapeDtypeStruct((T, 2 * LANES), BF16), tok(2 * LANES)),
        (jax.ShapeDtypeStruct((T, 2 * D_MODEL), BF16), tok(2 * D_MODEL)),
    ]
    return pl.pallas_call(
        functools.partial(_proj_kernel, tk=tk),
        grid=(nblk,),
        in_specs=[tok(D_MODEL), _const_spec((1, D_MODEL)), pos, pos, _const_spec((D_MODEL, P_END)),
                  _const_spec((1, HD_ATT)), _const_spec((1, HD_ATT)), _const_spec((1, LANES)), _const_spec((1, LANES))],
        out_specs=[o[1] for o in outs],
        out_shape=[o[0] for o in outs],
        compiler_params=_cparams(("arbitrary",)),
        name="proj",
    )(x, norm_g, cos_t, sin_t, w_all, qg, kg, glo, ghi)


def _ret_kernel(gpow_ref, q_ref, k_ref, v_ref, dm_ref, cd_ref, ud_ref, *rest, has_init):
    if has_init:
        s0_ref, o_ref, sfin_ref, s_ref = rest
    else:
        o_ref, sfin_ref, s_ref = rest
    hd = pl.program_id(1)
    sc = pl.program_id(2)

    @pl.when(sc == 0)
    def _():
        if has_init:
            s_ref[...] = s0_ref[0, 0]
        else:
            s_ref[...] = jnp.zeros_like(s_ref)

    q = q_ref[...]
    k = k_ref[...]
    v = v_ref[...]
    s = s_ref[...]
    scores = lax.dot_general(q, k, (((1,), (1,)), ((), ())), preferred_element_type=F32) * dm_ref[0]
    o = jnp.dot(scores.astype(BF16), v, preferred_element_type=F32)
    cd = cd_ref[0]
    o = o + jnp.dot(q, s.astype(BF16), preferred_element_type=F32) * jnp.concatenate([cd] * (DV_RET // LANES), axis=1)
    ud = ud_ref[0]
    kd = k.astype(F32) * jnp.concatenate([ud] * (DK_RET // LANES), axis=1)
    s_new = s * gpow_ref[hd] + jnp.dot(kd.T.astype(BF16), v, preferred_element_type=F32)
    s_ref[...] = s_new
    o_ref[...] = (o * lax.rsqrt(jnp.mean(o * o, axis=-1, keepdims=True) + EPS)).astype(BF16)

    @pl.when(sc == pl.num_programs(2) - 1)
    def _():
        sfin_ref[0, 0] = s_new


def _ret_tables(C, n_real):
    lg = np.log(1.0 - 2.0 ** (-5.0 - np.arange(H_RET, dtype=np.float64)))
    i = np.arange(C)
    real = i < n_real
    same_or_earlier = (i[None, :] // CHUNK) <= (i[:, None] // CHUNK)
    mask = same_or_earlier & real[None, :] & real[:, None]
    dm = np.exp(lg[:, None, None] * np.abs(i[:, None] - i[None, :])[None]) * mask[None]
    cd = np.exp(lg[:, None] * (i[None, :] + 1.0)) * real[None]
    ud = np.exp(lg[:, None] * (n_real - 1.0 - i)[None, :]) * real[None]
    gp = np.exp(lg * n_real)
    rep = lambda a: np.repeat(a[:, :, None], LANES, axis=2)
    return (jnp.asarray(gp, F32), jnp.asarray(dm, F32), jnp.asarray(rep(cd), F32), jnp.asarray(rep(ud), F32))


def _ret_call(rq, rk, rv, s0, *, B, nsc, C, n_real):
    T = rq.shape[0]
    gp, dm, cd, ud = _ret_tables(C, n_real)
    has_init = s0 is not None
    qk_spec = pl.BlockSpec((C, DK_RET), lambda b, h, s, *_: (b * nsc + s, h))
    v_spec = pl.BlockSpec((C, DV_RET), lambda b, h, s, *_: (b * nsc + s, h))
    tab = lambda w: pl.BlockSpec((1, C, w), lambda b, h, s, *_: (h, 0, 0))
    st_spec = pl.BlockSpec((1, 1, DK_RET, DV_RET), lambda b, h, s, *_: (b, h, 0, 0))
    in_specs = [qk_spec, qk_spec, v_spec, tab(C), tab(LANES), tab(LANES)]
    args = [rq, rk, rv, dm, cd, ud]
    if has_init:
        in_specs.append(st_spec)
        args.append(s0)
    return pl.pallas_call(
        functools.partial(_ret_kernel, has_init=has_init),
        grid_spec=pltpu.PrefetchScalarGridSpec(
            num_scalar_prefetch=1, grid=(B, H_RET, nsc), in_specs=in_specs,
            out_specs=[v_spec, st_spec],
            scratch_shapes=[pltpu.VMEM((DK_RET, DV_RET), F32)]),
        out_shape=[jax.ShapeDtypeStruct((T, H_RET * DV_RET), BF16),
                   jax.ShapeDtypeStruct((B, H_RET, DK_RET, DV_RET), F32)],
        compiler_params=_cparams(("arbitrary", "arbitrary", "arbitrary")),
        name="retention",
    )(gp, *args)


def _dsa_kernel(nt_ref, qpos_ref, aq_ref, iq_ref, ikw_ref, ik2_ref, k_ref, vt_ref, o_ref,
                keys_ref, m_ref, l_ref, acc_ref, *, qb, tk, topk):
    nt = nt_ref[pl.program_id(1)]
    qchunk = qpos_ref[0] >> 6
    wt = ikw_ref[...].T
    iq = iq_ref[...]
    iqs = jnp.concatenate([iq[:, p * LANES:(p + 1) * LANES] for p in range(H_IDX // 2)], axis=0)
    nt_dims = (((1,), (1,)), ((), ()))

    def tile_off(j):
        return pl.multiple_of(j * tk, tk)

    def p1(j, c):
        off = tile_off(j)
        ikt = ik2_ref[pl.ds(off, tk), :]
        r = jnp.concatenate([ikt[:, :LANES], ikt[:, LANES:]], axis=0)
        s = lax.dot_general(r, iqs, nt_dims, preferred_element_type=F32)
        score = jnp.zeros((tk, qb), F32)
        for h in range(H_IDX):
            sh = s[(h % 2) * tk:(h % 2 + 1) * tk, (h // 2) * qb:(h // 2 + 1) * qb]
            score = score + jnp.maximum(sh, 0.0) * wt[D_IDX + h:D_IDX + h + 1, :]
        bits = lax.bitcast_convert_type(score, I32)
        key = bits ^ ((bits >> 31) & 0x7FFFFFFF)
        kpos = off + lax.broadcasted_iota(I32, (tk, qb), 0)
        keys_ref[pl.ds(off, tk), :] = jnp.where((kpos >> 6) <= qchunk, key, INT_MIN)
        return c

    lax.fori_loop(0, nt, p1, 0)

    def count(pred):
        def body(j, acc):
            off = tile_off(j)
            blk = keys_ref[pl.ds(off, tk), :]
            kidx = off + lax.broadcasted_iota(I32, (tk, qb), 0)
            c = jnp.where(pred(blk, kidx), 1, 0)
            return acc + jnp.sum(c.reshape(tk // 8, 8, qb), axis=0)
        acc = lax.fori_loop(0, nt, body, jnp.zeros((8, qb), I32))
        return jnp.sum(acc, axis=0, keepdims=True)

    def bis(i, res):
        cand = res | lax.shift_left(jnp.int32(1), 31 - i)
        t = cand ^ INT_MIN
        c = count(lambda blk, kidx: blk >= t)
        return jnp.where(c >= topk, cand, res)

    res = lax.fori_loop(0, 32, bis, jnp.zeros((1, qb), I32))
    thr = jnp.maximum(res ^ INT_MIN, INT_MIN + 1)
    need = topk - count(lambda blk, kidx: blk > thr)
    n_eq = count(lambda blk, kidx: blk == thr)
    m_ref[0, 0:1, 0:qb] = jnp.full((1, qb), 2.0 ** 30, F32)

    @pl.when(jnp.max(jnp.where(n_eq > need, 1, 0)) > 0)
    def _():
        def jb(i, jl):
            cand = jl | lax.shift_left(jnp.int32(1), 14 - i)
            c = count(lambda blk, kidx: (blk == thr) & (kidx < cand))
            return jnp.where(c < need, cand, jl)
        jl = lax.fori_loop(0, 15, jb, jnp.zeros((1, qb), I32))
        m_ref[0, 0:1, 0:qb] = jl.astype(F32)

    jlast = m_ref[0, 0:1, 0:qb].astype(I32)

    m_ref[...] = jnp.full(m_ref.shape, NEG_BIG, F32)
    l_ref[...] = jnp.zeros(l_ref.shape, F32)
    acc_ref[...] = jnp.zeros(acc_ref.shape, F32)
    aq = aq_ref[...]
    gsz = H_ATT // N_KV
    qs = [jnp.concatenate([aq[:, (g * gsz + hh) * HD_ATT:(g * gsz + hh + 1) * HD_ATT] for hh in range(gsz)], axis=0)
          for g in range(N_KV)]

    def p3(j, c):
        off = tile_off(j)
        key = keys_ref[pl.ds(off, tk), :]
        kidx = off + lax.broadcasted_iota(I32, (tk, qb), 0)
        sel = (key > thr) | ((key == thr) & (kidx <= jlast))
        bias = jnp.where(sel, 0.0, -jnp.inf)
        bias = jnp.concatenate([bias] * gsz, axis=1)
        for g in range(N_KV):
            kt = k_ref[pl.ds(off, tk), g * HD_ATT:(g + 1) * HD_ATT]
            lt = lax.dot_general(kt, qs[g], nt_dims, preferred_element_type=F32) + bias
            m_old = m_ref[g]
            m_new = jnp.maximum(m_old, jnp.max(lt, axis=0, keepdims=True))
            alpha = jnp.exp2(m_old - m_new)
            p = jnp.exp2(lt - m_new)
            l_ref[g] = alpha * l_ref[g] + jnp.sum(p, axis=0, keepdims=True)
            vt = vt_ref[j, g * HD_ATT:(g + 1) * HD_ATT, :]
            acc_ref[g] = alpha * acc_ref[g] + jnp.dot(vt, p.astype(BF16), preferred_element_type=F32)
            m_ref[g] = m_new
        return c

    lax.fori_loop(0, nt, p3, 0)
    for g in range(N_KV):
        o = acc_ref[g] / jnp.maximum(l_ref[g], 1e-30)
        for hh in range(gsz):
            o_ref[:, (g * gsz + hh) * HD_ATT:(g * gsz + hh + 1) * HD_ATT] = o[:, hh * qb:(hh + 1) * qb].T.astype(BF16)


def _dsa_call(ntiles, qpos, aq, iq, ikw, ik2, akb, avt, *, B, nqb, qb, lk, tk, topk):
    gsz = H_ATT // N_KV
    qrow = lambda w: pl.BlockSpec((qb, w), lambda b, i, *_: (b * nqb + i, 0))
    krow = lambda w: pl.BlockSpec((lk, w), lambda b, i, *_: (b, 0), pipeline_mode=pl.Buffered(1))
    return pl.pallas_call(
        functools.partial(_dsa_kernel, qb=qb, tk=tk, topk=topk),
        grid_spec=pltpu.PrefetchScalarGridSpec(
            num_scalar_prefetch=1, grid=(B, nqb),
            in_specs=[pl.BlockSpec((1, 1, qb), lambda b, i, *_: (i, 0, 0)),
                      qrow(H_ATT * HD_ATT), qrow(H_IDX * D_IDX), qrow(LANES),
                      krow(2 * LANES), krow(N_KV * HD_ATT),
                      pl.BlockSpec((lk // tk, N_KV * HD_ATT, tk), lambda b, i, *_: (b, 0, 0),
                                   pipeline_mode=pl.Buffered(1))],
            out_specs=qrow(H_ATT * HD_ATT),
            scratch_shapes=[pltpu.VMEM((lk, qb), I32),
                            pltpu.VMEM((N_KV, 1, gsz * qb), F32),
                            pltpu.VMEM((N_KV, 1, gsz * qb), F32),
                            pltpu.VMEM((N_KV, HD_ATT, gsz * qb), F32)]),
        out_shape=jax.ShapeDtypeStruct((B * nqb * qb, H_ATT * HD_ATT), BF16),
        compiler_params=_cparams(("arbitrary", "arbitrary")),
        name="dsa",
    )(ntiles, qpos, aq, iq, ikw, ik2, akb, avt)


def _merge_kernel(x_ref, oret_ref, rg_ref, oatt_ref, gt_ref, wro_ref, wao_ref, wo_ref, gf_ref, wr_ref, br_ref,
                  *rest, nblk, has_tail):
    if has_tail:
        tail_ref, x1_ref, hn_ref, te_ref, tg_ref = rest
    else:
        x1_ref, hn_ref, te_ref, tg_ref = rest

    def body():
        rg = rg_ref[...].astype(F32)
        a = (rg * jax.nn.sigmoid(rg) * oret_ref[...].astype(F32)).astype(BF16)
        ret_b = jnp.dot(a, wro_ref[...], preferred_element_type=F32)
        att_b = jnp.dot(oatt_ref[...], wao_ref[...], preferred_element_type=F32)
        gt = gt_ref[...].astype(F32)
        mixed = jax.nn.sigmoid(gt[:, :D_MODEL]) * ret_b + jax.nn.sigmoid(gt[:, D_MODEL:]) * att_b
        x1 = x_ref[...] + jnp.dot(mixed.astype(BF16), wo_ref[...], preferred_element_type=F32)
        x1_ref[...] = x1
        hn = x1 * lax.rsqrt(jnp.mean(x1 * x1, axis=-1, keepdims=True) + EPS) * gf_ref[...]
        hn_ref[...] = hn
        logits = jnp.dot(hn.astype(BF16), wr_ref[...], preferred_element_type=F32) + br_ref[...]
        lane = lax.broadcasted_iota(I32, logits.shape, 1)
        cur = jnp.where(lane < N_EXP, logits, -jnp.inf)
        te = jnp.zeros(logits.shape, I32)
        tv = jnp.zeros(logits.shape, F32)
        for kk in range(TOP_K):
            mx = jnp.max(cur, axis=-1, keepdims=True)
            idx = jnp.min(jnp.where(cur == mx, lane, LANES), axis=-1, keepdims=True)
            te = jnp.where(lane == kk, idx, te)
            tv = jnp.where(lane == kk, mx, tv)
            cur = jnp.where(lane == idx, -jnp.inf, cur)
        tvm = jnp.where(lane < TOP_K, tv, -jnp.inf)
        e = jnp.exp(tvm - jnp.max(tvm, axis=-1, keepdims=True))
        te_ref[...] = te
        tg_ref[...] = e / jnp.sum(e, axis=-1, keepdims=True)

    if has_tail:
        i = pl.program_id(0)
        pl.when(i < nblk)(body)

        @pl.when(i >= nblk)
        def _():
            hn_ref[...] = tail_ref[...]
    else:
        body()


def _merge_call(x, oret, rg, oatt, gates, wro, wao, wo, gf, wr, br, hn_tail, *, tm):
    T = x.shape[0]
    nblk = T // tm
    has_tail = hn_tail is not None
    ntail = hn_tail.shape[0] // tm if has_tail else 0
    assert not has_tail or hn_tail.shape[0] % tm == 0
    tok = lambda w: pl.BlockSpec((tm, w), lambda i: (jnp.minimum(i, nblk - 1), 0))
    in_specs = [tok(D_MODEL), tok(H_RET * DV_RET), tok(H_RET * DV_RET), tok(H_ATT * HD_ATT), tok(2 * D_MODEL),
                _const_spec((H_RET * DV_RET, D_MODEL)), _const_spec((H_ATT * HD_ATT, D_MODEL)),
                _const_spec((D_MODEL, D_MODEL)), _const_spec((1, D_MODEL)),
                _const_spec((D_MODEL, LANES)), _const_spec((1, LANES))]
    args = [x, oret, rg, oatt, gates, wro, wao, wo, gf, wr, br]
    if has_tail:
        in_specs.append(pl.BlockSpec((tm, D_MODEL), lambda i: (jnp.maximum(i - nblk, 0), 0)))
        args.append(hn_tail)
    return pl.pallas_call(
        functools.partial(_merge_kernel, nblk=nblk, has_tail=has_tail),
        grid=(nblk + ntail,),
        in_specs=in_specs,
        out_specs=[tok(D_MODEL), pl.BlockSpec((tm, D_MODEL), lambda i: (i, 0)), tok(LANES), tok(LANES)],
        out_shape=[jax.ShapeDtypeStruct((T, D_MODEL), F32), jax.ShapeDtypeStruct((T + ntail * tm, D_MODEL), F32),
                   jax.ShapeDtypeStruct((T, LANES), I32), jax.ShapeDtypeStruct((T, LANES), F32)],
        compiler_params=_cparams(("arbitrary",)),
        name="merge",
    )(*args)


def _experts_kernel(be_ref, src0_ref, srcn_ref, dst_ref, hn_ref, wg_ref, bg_ref, wu_ref, bu_ref, wd_ref, bd_ref,
                    ytk_ref, xbuf, xb, ybuf, gsem, ssem, *, tm, dump0):
    i = pl.program_id(0)
    nb = pl.num_programs(0)

    def start_gather(src_ref):
        for r in range(tm):
            pltpu.make_async_copy(hn_ref.at[pl.ds(src_ref[0, 0, r], 1)], xbuf.at[pl.ds(r, 1)], gsem.at[0]).start()

    def wait_gather():
        pltpu.make_async_copy(hn_ref.at[pl.ds(0, tm)], xbuf, gsem.at[0]).wait()

    def start_scatter(dst_row):
        for r in range(tm):
            pltpu.make_async_copy(ybuf.at[pl.ds(r, 1)], ytk_ref.at[pl.ds(dst_row(r), 1)], ssem.at[0]).start()

    def wait_scatter():
        pltpu.make_async_copy(ybuf, ytk_ref.at[pl.ds(0, tm)], ssem.at[0]).wait()

    @pl.when(i == 0)
    def _():
        ybuf[...] = jnp.zeros_like(ybuf)
        start_scatter(lambda r: dump0 + r)
        start_gather(src0_ref)

    wait_gather()
    xb[...] = xbuf[...].astype(BF16)
    start_gather(srcn_ref)
    x = xb[...]
    a = jnp.minimum(jnp.dot(x, wg_ref[0], preferred_element_type=F32) + bg_ref[0], SWIGLU_LIMIT)
    u = jnp.clip(jnp.dot(x, wu_ref[0], preferred_element_type=F32) + bu_ref[0], -SWIGLU_LIMIT, SWIGLU_LIMIT)
    hid = (a * jax.nn.sigmoid(SWIGLU_ALPHA * a) * (u + 1.0)).astype(BF16)
    y = jnp.dot(hid, wd_ref[0], preferred_element_type=F32) + bd_ref[0]
    wait_scatter()
    ybuf[...] = y
    start_scatter(lambda r: dst_ref[0, 0, r])

    @pl.when(i == nb - 1)
    def _():
        wait_gather()
        wait_scatter()


def _experts_call(block_e, row_src, row_dst, hn_all, wg, bg, wu, bu, wd, bd, *, tm, n_rows_out):
    nb = row_src.shape[0]
    wspec = lambda k, n: pl.BlockSpec((1, k, n), lambda i, be: (be[i], 0, 0))
    smem = lambda imap: pl.BlockSpec((1, 1, tm), imap, memory_space=pltpu.SMEM)
    return pl.pallas_call(
        functools.partial(_experts_kernel, tm=tm, dump0=n_rows_out - tm),
        grid_spec=pltpu.PrefetchScalarGridSpec(
            num_scalar_prefetch=1, grid=(nb,),
            in_specs=[smem(lambda i, be: (0, 0, 0)),
                      smem(lambda i, be: (jnp.minimum(i + 1, nb - 1), 0, 0)),
                      smem(lambda i, be: (i, 0, 0)),
                      pl.BlockSpec(memory_space=pl.ANY),
                      wspec(D_MODEL, D_FF), wspec(1, D_FF), wspec(D_MODEL, D_FF), wspec(1, D_FF),
                      wspec(D_FF, D_MODEL), wspec(1, D_MODEL)],
            out_specs=pl.BlockSpec(memory_space=pl.ANY),
            scratch_shapes=[pltpu.VMEM((tm, D_MODEL), F32), pltpu.VMEM((tm, D_MODEL), BF16),
                            pltpu.VMEM((tm, D_MODEL), F32),
                            pltpu.SemaphoreType.DMA((1,)), pltpu.SemaphoreType.DMA((1,))]),
        out_shape=jax.ShapeDtypeStruct((n_rows_out, D_MODEL), F32),
        compiler_params=_cparams(("arbitrary",)),
        name="experts",
    )(block_e, row_src, row_src, row_dst, hn_all, wg, bg, wu, bu, wd, bd)


def _combine_kernel(x1_ref, y0_ref, y1_ref, y2_ref, y3_ref, tg_ref, y_ref):
    tg = tg_ref[...]
    y = x1_ref[...]
    for kk, yk in enumerate((y0_ref, y1_ref, y2_ref, y3_ref)):
        y = y + yk[...] * tg[:, kk:kk + 1]
    y_ref[...] = y


def _combine_call(x1, ytk, tg, *, tm, t_all, row0):
    T = x1.shape[0]
    assert t_all % tm == 0 and row0 % tm == 0
    off = row0 // tm
    nblk_all = t_all // tm
    tok = lambda w: pl.BlockSpec((tm, w), lambda i: (i, 0))
    yk = lambda k: pl.BlockSpec((tm, D_MODEL), lambda i: (k * nblk_all + off + i, 0))
    return pl.pallas_call(
        _combine_kernel,
        grid=(T // tm,),
        in_specs=[tok(D_MODEL)] + [yk(k) for k in range(TOP_K)] + [tok(LANES)],
        out_specs=tok(D_MODEL),
        out_shape=jax.ShapeDtypeStruct((T, D_MODEL), F32),
        compiler_params=_cparams(("arbitrary",)),
        name="combine",
    )(x1, ytk, ytk, ytk, ytk, tg)


def _pack_w_in(w_in):
    z = lambda n: jnp.zeros((D_MODEL, n), w_in.dtype)
    w_ik = w_in[:, C_IK:C_IW]
    w_iw = w_in[:, C_IW:C_GT]
    return jnp.concatenate([w_in[:, :C_IK], w_ik, w_iw, z(LANES - D_IDX - H_IDX), z(D_IDX), w_ik, w_in[:, C_GT:]],
                           axis=1).astype(BF16)


def _rope_tables(pos):
    half = DK_RET // 2
    inv_freq = ROPE_BASE ** (-jnp.arange(half, dtype=F32) / half)
    ang = pos.astype(F32)[:, None] * inv_freq[None, :]
    return jnp.cos(ang), jnp.sin(ang)


def _moe(hn_all, te, wg, bg, wu, bu, wd, bd, *, tm):
    T = hn_all.shape[0]
    A = T * TOP_K
    e_km = te[:, :TOP_K].T.reshape(A)
    counts = jnp.sum((e_km[:, None] == jnp.arange(N_EXP, dtype=I32)[None, :]).astype(I32), axis=0)
    order = jnp.argsort(e_km, stable=True).astype(I32)
    padded = (counts + tm - 1) // tm * tm
    pend = jnp.cumsum(padded)
    pstart = pend - padded
    start = jnp.cumsum(counts) - counts
    nb = -(-(A + N_EXP * (tm - 1)) // tm)
    blk0 = jnp.arange(nb, dtype=I32) * tm
    block_e = jnp.minimum(jnp.sum((pend[None, :] <= blk0[:, None]).astype(I32), axis=1), N_EXP - 1).astype(I32)
    p = jnp.arange(nb * tm, dtype=I32)
    e_p = jnp.repeat(block_e, tm)
    j = p - pstart[e_p]
    valid = j < counts[e_p]
    a = order[jnp.clip(start[e_p] + j, 0, A - 1)]
    row_src = jnp.where(valid, a % T, 0).astype(I32).reshape(nb, 1, tm)
    row_dst = jnp.where(valid, a, A + p % tm).astype(I32).reshape(nb, 1, tm)
    return _experts_call(block_e, row_src, row_dst, hn_all, wg, bg, wu, bu, wd, bd, tm=tm, n_rows_out=A + tm)


def kernel(x_prompt, x_sample, cache_att_k, cache_att_v, cache_idx_k, state_ret, norm_mix_g, w_in, q_norm_g, k_norm_g,
           idx_k_norm_g, w_ret_out, w_att_out, w_o, norm_ffn_g, w_router, b_router, w_gate, b_gate, w_up, b_up,
           w_down, b_down):
    return _forward(x_prompt, x_sample, cache_att_k, cache_att_v, cache_idx_k, state_ret, norm_mix_g, w_in, q_norm_g,
                    k_norm_g, idx_k_norm_g, w_ret_out, w_att_out, w_o, norm_ffn_g, w_router, b_router, w_gate,
                    b_gate, w_up, b_up, w_down, b_down)


def _forward(x_prompt, x_sample, cache_att_k, cache_att_v, cache_idx_k, state_ret, norm_mix_g, w_in, q_norm_g,
             k_norm_g, idx_k_norm_g, w_ret_out, w_att_out, w_o, norm_ffn_g, w_router, b_router, w_gate, b_gate,
             w_up, b_up, w_down, b_down, *, TM=512, TK=512, C=256, MT=512):
    assert norm_mix_g.shape[0] == 1, "single layer"
    B, L, _ = x_prompt.shape
    Bs, n_s, _ = x_sample.shape
    past = cache_att_k.shape[2]
    QB = LANES

    w_all = _pack_w_in(w_in[0])
    g_mix = norm_mix_g[0][None, :]
    qg = q_norm_g[0][None, :]
    kg = k_norm_g[0][None, :]
    zpad = jnp.zeros((D_IDX,), F32)
    glo = jnp.concatenate([idx_k_norm_g[0], zpad])[None, :]
    ghi = jnp.concatenate([zpad, idx_k_norm_g[0]])[None, :]
    wro = w_ret_out[0].astype(BF16)
    wao = w_att_out[0].astype(BF16)
    wo = w_o[0].astype(BF16)
    gf = norm_ffn_g[0][None, :]
    wr = jnp.pad(w_router[0], ((0, 0), (0, LANES - N_EXP))).astype(BF16)
    br = jnp.pad(b_router[0], (0, LANES - N_EXP))[None, :]
    wg, wu, wd = w_gate[0].astype(BF16), w_up[0].astype(BF16), w_down[0].astype(BF16)
    bg, bu, bd = b_gate[0][:, None, :], b_up[0][:, None, :], b_down[0][:, None, :]

    Tp = B * L
    xp = x_prompt.reshape(Tp, D_MODEL)
    cos_p, sin_p = _rope_tables(jnp.arange(L, dtype=I32))
    (rq, rk, rv, rg, aq, ak, av, akb, avt, iq, ikw, ik2, gates) = _proj_call(
        xp, g_mix, cos_p, sin_p, w_all, qg, kg, glo, ghi, tm=TM, tk=TK, npos=L // TM)
    oret, s_p = _ret_call(rq, rk, rv, None, B=B, nsc=L // C, C=C, n_real=C)
    nqb = L // QB
    qpos = jnp.arange(L, dtype=I32).reshape(nqb, 1, QB)
    lim = (qpos[:, 0, -1] // CHUNK + 1) * CHUNK
    ntiles = ((lim + TK - 1) // TK).astype(I32)
    topk = min(TOPK_MAX, L // 4)
    oatt = _dsa_call(ntiles, qpos, aq, iq, ikw, ik2, akb, avt, B=B, nqb=nqb, qb=QB, lk=L, tk=TK, topk=topk)

    Ts = Bs * n_s
    xs_ = x_sample.reshape(Ts, D_MODEL)
    TMs = min(TM, Ts)
    cos_s, sin_s = _rope_tables(past + jnp.arange(n_s, dtype=I32))
    rep = TMs // n_s
    TKs = 128
    (rq_s, rk_s, rv_s, rg_s, aq_s, ak_s, av_s, akb_s, _, iq_s, ikw_s, ik2_s, gates_s) = _proj_call(
        xs_, g_mix, jnp.tile(cos_s, (rep, 1)), jnp.tile(sin_s, (rep, 1)), w_all, qg, kg, glo, ghi,
        tm=TMs, tk=TKs, npos=1)
    padq = lambda a: jnp.pad(a.reshape(Bs, n_s, -1), ((0, 0), (0, QB - n_s), (0, 0))).reshape(Bs * QB, -1)
    oret_s, s_s = _ret_call(padq(rq_s), padq(rk_s), padq(rv_s), state_ret[0], B=Bs, nsc=1, C=QB, n_real=n_s)
    oret_s = oret_s.reshape(Bs, QB, -1)[:, :n_s].reshape(Ts, -1)
    lk_true = past + n_s
    lk_s = -(-lk_true // TK) * TK
    kpad = lambda a: jnp.pad(a, ((0, 0), (0, lk_s - lk_true), (0, 0)))
    k_all = kpad(jnp.concatenate([cache_att_k[0].reshape(Bs, past, -1).astype(BF16),
                                  akb_s.reshape(Bs, n_s, -1)], axis=1))
    v_all = kpad(jnp.concatenate([cache_att_v[0].reshape(Bs, past, -1).astype(BF16),
                                  av_s.reshape(Bs, n_s, -1).astype(BF16)], axis=1))
    cik = cache_idx_k[0].astype(BF16)
    zc = jnp.zeros_like(cik)
    ik2_all = kpad(jnp.concatenate([jnp.concatenate([cik, zc, zc, cik], axis=-1),
                                    ik2_s.reshape(Bs, n_s, -1)], axis=1))
    avt_all = v_all.reshape(Bs, lk_s // TK, TK, -1).transpose(0, 1, 3, 2).reshape(Bs * (lk_s // TK), -1, TK)
    qpos_s = jnp.concatenate([past + jnp.arange(n_s, dtype=I32), jnp.full((QB - n_s,), -1, I32)]).reshape(1, 1, QB)
    ntiles_s = jnp.full((1,), lk_s // TK, I32)
    topk_s = min(TOPK_MAX, lk_true // 4)
    oatt_s = _dsa_call(ntiles_s, qpos_s, padq(aq_s), padq(iq_s), padq(ikw_s),
                       ik2_all.reshape(Bs * lk_s, -1), k_all.reshape(Bs * lk_s, -1), avt_all,
                       B=Bs, nqb=1, qb=QB, lk=lk_s, tk=TK, topk=topk_s)
    oatt_s = oatt_s.reshape(Bs, QB, -1)[:, :n_s].reshape(Ts, -1)
    x1s, hn_s, tes, tgs = _merge_call(xs_, oret_s, rg_s, oatt_s, gates_s, wro, wao, wo, gf, wr, br, None, tm=TMs)
    assert TMs == TM or Ts % TM == 0
    x1p, hn_all, tep, tgp = _merge_call(xp, oret, rg, oatt, gates, wro, wao, wo, gf, wr, br, hn_s, tm=TM)
    T_all = Tp + Ts

    ytk = _moe(hn_all, jnp.concatenate([tep, tes], axis=0), wg, bg, wu, bu, wd, bd, tm=MT)
    yp = _combine_call(x1p, ytk, tgp, tm=TM, t_all=T_all, row0=0)
    ys = _combine_call(x1s, ytk, tgs, tm=TMs, t_all=T_all, row0=Tp)

    return (yp.reshape(B, L, D_MODEL), ys.reshape(Bs, n_s, D_MODEL),
            ak.reshape(1, B, L, N_KV, HD_ATT), av.reshape(1, B, L, N_KV, HD_ATT),
            ikw[:, :D_IDX].reshape(1, B, L, D_IDX), s_p[None],
            ak_s.reshape(1, Bs, n_s, N_KV, HD_ATT), av_s.reshape(1, Bs, n_s, N_KV, HD_ATT),
            ikw_s[:, :D_IDX].reshape(1, Bs, n_s, D_IDX), s_s[None])
```
